```python
import math
import jax, jax.numpy as jnp
from jax import lax
import numpy as np

D_MODEL = 1024
BATCH = 8
SEQ = 2048
DEPTH = 2
DEC_BATCH = 128
DEC_SEQ = 1
PAST_LEN = 16384
PAGE_SIZE = 128

N_MIXERS = 2
N_LRU_LAYERS = (DEPTH + 1) // 2
N_MLSTM_LAYERS = DEPTH // 2
CONV_W = 4
D_RNN = D_MODEL
LRU_BLOCKS = 16
LRU_BW = D_RNN // LRU_BLOCKS
LRU_C = 8.0
M_INNER = 2 * D_MODEL
M_HEADS = 4
M_HD = M_INNER // M_HEADS
QKV_BS = 4
M_CHUNK = 64
N_KEYS = 128
N_EXPERTS = N_KEYS * N_KEYS
P_HEADS = 8
P_DKEY = 256
P_HALF = P_DKEY // 2
P_TOPK = 16
PEER_BLOCK = 256
EPS = 1e-6

kernel_name = 'hybrid_rglru_mlstm_peer_adaln_step'


def rmsnorm(x, g):
    xf = x.astype(jnp.float32)
    return xf * lax.rsqrt(jnp.mean(xf * xf, axis=-1, keepdims=True) + EPS) * g.astype(jnp.float32)


def causal_conv(x, buf, w, b):
    T = x.shape[1]
    xp = jnp.concatenate([buf.astype(x.dtype), x], axis=1)
    y = b.astype(x.dtype)
    for j in range(CONV_W):
        y = y + w[j] * xp[:, j:j + T]
    return y, xp[:, -(CONV_W - 1):]


def blockdiag(x, w):
    nb, bi, bo = w.shape
    xb = x.reshape(*x.shape[:-1], nb, bi)
    return jnp.einsum('...gi,gio->...go', xb, w).reshape(*x.shape[:-1], nb * bo)


def rglru_mixer(h, h0, conv_buf, w_in, conv_w, conv_b, w_a, b_a, w_i, b_i, lam, w_out):
    proj = h @ w_in
    y_br = jax.nn.gelu(proj[..., :D_RNN])
    xr, new_buf = causal_conv(proj[..., D_RNN:], conv_buf, conv_w, conv_b)
    r = jax.nn.sigmoid(blockdiag(xr, w_a) + b_a)
    i = jax.nn.sigmoid(blockdiag(xr, w_i) + b_i)
    log_a = LRU_C * r * jax.nn.log_sigmoid(lam.astype(jnp.float32))
    a = jnp.exp(log_a)
    bx = jnp.sqrt(-jnp.expm1(2.0 * log_a)) * (i * xr)
    bx = bx.at[:, 0].add(a[:, 0] * h0.astype(jnp.float32))

    def comb(lhs, rhs):
        a1, b1 = lhs
        a2, b2 = rhs
        return a1 * a2, a2 * b1 + b2

    _, hs = lax.associative_scan(comb, (a, bx), axis=1)
    out = (hs * y_br) @ w_out
    return out, hs[:, -1], new_buf


def mlstm_cell(q, k, v, i_pre, logf, C0, n0, m0):
    B, T, H, Dh = q.shape
    L = math.gcd(T, M_CHUNK)
    nc = T // L

    def chunks(t):
        t = t.reshape(B, nc, L, H, *t.shape[3:])
        return jnp.moveaxis(t, (1, 3), (0, 2))

    mask = jnp.tril(jnp.ones((L, L), dtype=bool))

    def step(carry, inp):
        C, n, m = carry
        qc, kc, vc, ic, fc = inp
        b = jnp.cumsum(fc, axis=-1)
        dmat = jnp.where(mask, b[..., :, None] - b[..., None, :] + ic[..., None, :], -jnp.inf)
        m_inter = b + m[..., None]
        m_j = jnp.maximum(m_inter, dmat.max(-1))
        w = jnp.exp(dmat - m_j[..., None])
        s_inter = jnp.exp(m_inter - m_j)
        s = jnp.einsum('bhjd,bhsd->bhjs', qc, kc) * w
        num = s_inter[..., None] * jnp.einsum('bhjd,bhde->bhje', qc, C) + jnp.einsum('bhjs,bhse->bhje', s, vc)
        den = s_inter * jnp.einsum('bhjd,bhd->bhj', qc, n) + s.sum(-1)
        hout = num / jnp.maximum(jnp.abs(den), jnp.exp(-m_j))[..., None]
        g = b[..., -1:] - b + ic
        m_new = jnp.maximum(b[..., -1] + m, g.max(-1))
        wg = jnp.exp(g - m_new[..., None])
        decay = jnp.exp(b[..., -1] + m - m_new)
        C_new = decay[..., None, None] * C + jnp.einsum('bhs,bhsd,bhse->bhde', wg, kc, vc)
        n_new = decay[..., None] * n + jnp.einsum('bhs,bhsd->bhd', wg, kc)
        return (C_new, n_new, m_new), hout

    carry0 = (C0.astype(jnp.float32), n0.astype(jnp.float32), m0.astype(jnp.float32))
    (C, n, m), hs = lax.scan(step, carry0, (chunks(q), chunks(k), chunks(v), chunks(i_pre), chunks(logf)))
    hs = jnp.moveaxis(hs, (0, 2), (1, 3)).reshape(B, T, H, Dh)
    return hs, C, n, m


def mlstm_mixer(h, C0, n0, m0, conv_buf, w_in, conv_w, conv_b, w_q, w_k, w_v, w_ig, b_ig, w_fg, b_fg, skip, ln_g, w_out):
    B, T, _ = h.shape
    proj = h @ w_in
    xm, z = proj[..., :M_INNER], proj[..., M_INNER:]
    xc, new_buf = causal_conv(xm, conv_buf, conv_w, conv_b)
    xc = jax.nn.silu(xc)
    q = blockdiag(xc, w_q)
    k = blockdiag(xc, w_k) * (M_HD ** -0.5)
    v = blockdiag(xm, w_v)
    qkv = jnp.concatenate([q, k, v], axis=-1)
    i_pre = qkv @ w_ig + b_ig
    logf = jax.nn.log_sigmoid(qkv @ w_fg + b_fg)
    heads = lambda t: t.reshape(B, T, M_HEADS, M_HD)
    hc, C, n, m = mlstm_cell(heads(q), heads(k), heads(v), i_pre, logf, C0, n0, m0)
    mu = hc.mean(-1, keepdims=True)
    var = jnp.mean(jnp.square(hc - mu), axis=-1, keepdims=True)
    hn = ((hc - mu) * lax.rsqrt(var + EPS)).reshape(B, T, M_INNER) * ln_g
    out = ((hn + skip * xc) * jax.nn.silu(z)) @ w_out
    return out, C, n, m, new_buf


def peer_ffn(h, w_q, sub_keys, u, v):
    B, T, D = h.shape
    n_tok = B * T
    nb = -(-n_tok // PEER_BLOCK)
    xt = jnp.pad(h.reshape(n_tok, D), ((0, nb * PEER_BLOCK - n_tok), (0, 0))).reshape(nb, PEER_BLOCK, D)

    def block(xb):
        q = (xb @ w_q).reshape(PEER_BLOCK, P_HEADS, 2, P_HALF)
        s = jnp.einsum('thpd,hpkd->thpk', q, sub_keys).astype(jnp.float32)
        s1, i1 = lax.top_k(s[:, :, 0], P_TOPK)
        s2, i2 = lax.top_k(s[:, :, 1], P_TOPK)
        cand = (s1[..., :, None] + s2[..., None, :]).reshape(PEER_BLOCK, P_HEADS, P_TOPK * P_TOPK)
        cidx = (i1[..., :, None] * N_KEYS + i2[..., None, :]).reshape(PEER_BLOCK, P_HEADS, P_TOPK * P_TOPK)
        top, pos = lax.top_k(cand, P_TOPK)
        idx = jnp.take_along_axis(cidx, pos, axis=-1)
        g = jax.nn.softmax(top, axis=-1)
        act = jax.nn.gelu(jnp.einsum('td,thkd->thk', xb, u[idx]))
        return jnp.einsum('thk,thkd->td', g * act, v[idx])

    out = lax.map(block, xt).reshape(nb * PEER_BLOCK, D)[:n_tok]
    return out.reshape(B, T, D)


def trunk(x, c, lru_h, lru_conv, ml_C, ml_n, ml_m, ml_conv, p):
    out_dtype = x.dtype
    x = x.astype(jnp.float32)
    c = c.astype(jnp.float32)
    new_lh, new_lc, new_C, new_n, new_m, new_mc = [], [], [], [], [], []
    for l in range(DEPTH):
        mod = (c @ p['w_ada'][l] + p['b_ada'][l])[:, None, :]
        sh1, sc1, g1, sh2, sc2, g2 = jnp.split(mod, 6, axis=-1)
        hm = rmsnorm(x, p['norm_mix'][l]) * (1.0 + sc1) + sh1
        j = l // N_MIXERS
        if l % N_MIXERS == 0:
            mix, hT, cb = rglru_mixer(hm, lru_h[j], lru_conv[j], p['lru_w_in'][j], p['lru_conv_w'][j], p['lru_conv_b'][j],
                                      p['lru_w_a'][j], p['lru_b_a'][j], p['lru_w_i'][j], p['lru_b_i'][j],
                                      p['lru_lambda'][j], p['lru_w_out'][j])
            new_lh.append(hT)
            new_lc.append(cb)
        else:
            mix, C, n, m, cb = mlstm_mixer(hm, ml_C[j], ml_n[j], ml_m[j], ml_conv[j], p['ml_w_in'][j], p['ml_conv_w'][j],
                                           p['ml_conv_b'][j], p['ml_w_q'][j], p['ml_w_k'][j], p['ml_w_v'][j],
                                           p['ml_w_ig'][j], p['ml_b_ig'][j], p['ml_w_fg'][j], p['ml_b_fg'][j],
                                           p['ml_skip'][j], p['ml_ln_g'][j], p['ml_w_out'][j])
            new_C.append(C)
            new_n.append(n)
            new_m.append(m)
            new_mc.append(cb)
        x = x + g1 * mix
        hf = rmsnorm(x, p['norm_ffn'][l]) * (1.0 + sc2) + sh2
        x = x + g2 * peer_ffn(hf, p['peer_w_q'][l], p['peer_keys'][l], p['peer_u'][l], p['peer_v'][l])
    y = rmsnorm(x, p['norm_f']).astype(out_dtype)
    return y, (jnp.stack(new_lh), jnp.stack(new_lc), jnp.stack(new_C), jnp.stack(new_n), jnp.stack(new_m), jnp.stack(new_mc))


def setup_inputs(seed: int = 0) -> dict:
    key = jax.random.key(seed)
    ks = list(jax.random.split(key, 64))

    def nrm(shape, scale):
        return jax.random.normal(ks.pop(), shape, jnp.float32) * scale

    D = D_MODEL
    NA, NB = N_LRU_LAYERS, N_MLSTM_LAYERS
    u_lam = jax.random.uniform(ks.pop(), (NA, D_RNN), jnp.float32, 0.9, 0.999)
    a_base = u_lam ** (1.0 / LRU_C)
    lru_lambda = jnp.log(a_base) - jnp.log1p(-a_base)
    fg_bias = jnp.broadcast_to(jnp.linspace(3.0, 6.0, M_HEADS, dtype=jnp.float32), (NB, M_HEADS)) + nrm((NB, M_HEADS), 0.1)
    return {
        'x_prompt': nrm((BATCH, SEQ, D), 1.0),
        'x_sample': nrm((DEC_BATCH, DEC_SEQ, D), 1.0),
        'c_prompt': nrm((BATCH, D), 1.0),
        'c_sample': nrm((DEC_BATCH, D), 1.0),
        'state_lru_h': nrm((NA, DEC_BATCH, D_RNN), 0.5),
        'state_lru_conv': nrm((NA, DEC_BATCH, CONV_W - 1, D_RNN), 1.0),
        'state_mlstm_C': nrm((NB, DEC_BATCH, M_HEADS, M_HD, M_HD), M_HD ** -0.5),
        'state_mlstm_n': nrm((NB, DEC_BATCH, M_HEADS, M_HD), 0.1),
        'state_mlstm_m': nrm((NB, DEC_BATCH, M_HEADS), 0.5),
        'state_mlstm_conv': nrm((NB, DEC_BATCH, CONV_W - 1, M_INNER), 1.0),
        'w_ada': nrm((DEPTH, D, 6 * D), 0.3 * D ** -0.5),
        'b_ada': nrm((DEPTH, 6 * D), 0.02),
        'norm_mix': 1.0 + nrm((DEPTH, D), 0.02),
        'norm_ffn': 1.0 + nrm((DEPTH, D), 0.02),
        'norm_f': 1.0 + nrm((D,), 0.02),
        'lru_w_in': nrm((NA, D, 2 * D_RNN), D ** -0.5),
        'lru_conv_w': nrm((NA, CONV_W, D_RNN), 0.5),
        'lru_conv_b': nrm((NA, D_RNN), 0.02),
        'lru_w_a': nrm((NA, LRU_BLOCKS, LRU_BW, LRU_BW), LRU_BW ** -0.5),
        'lru_b_a': nrm((NA, D_RNN), 0.02),
        'lru_w_i': nrm((NA, LRU_BLOCKS, LRU_BW, LRU_BW), LRU_BW ** -0.5),
        'lru_b_i': nrm((NA, D_RNN), 0.02),
        'lru_lambda': lru_lambda,
        'lru_w_out': nrm((NA, D_RNN, D), D_RNN ** -0.5),
        'ml_w_in': nrm((NB, D, 2 * M_INNER), D ** -0.5),
        'ml_conv_w': nrm((NB, CONV_W, M_INNER), 0.5),
        'ml_conv_b': nrm((NB, M_INNER), 0.02),
        'ml_w_q': nrm((NB, M_INNER // QKV_BS, QKV_BS, QKV_BS), QKV_BS ** -0.5),
        'ml_w_k': nrm((NB, M_INNER // QKV_BS, QKV_BS, QKV_BS), QKV_BS ** -0.5),
        'ml_w_v': nrm((NB, M_INNER // QKV_BS, QKV_BS, QKV_BS), QKV_BS ** -0.5),
        'ml_w_ig': nrm((NB, 3 * M_INNER, M_HEADS), 0.5 * (3 * M_INNER) ** -0.5),
        'ml_b_ig': nrm((NB, M_HEADS), 0.1),
        'ml_w_fg': nrm((NB, 3 * M_INNER, M_HEADS), 0.5 * (3 * M_INNER) ** -0.5),
        'ml_b_fg': fg_bias,
        'ml_skip': 1.0 + nrm((NB, M_INNER), 0.02),
        'ml_ln_g': 1.0 + nrm((NB, M_INNER), 0.02),
        'ml_w_out': nrm((NB, M_INNER, D), M_INNER ** -0.5),
        'peer_w_q': nrm((DEPTH, D, P_HEADS * P_DKEY), D ** -0.5),
        'peer_keys': nrm((DEPTH, P_HEADS, 2, N_KEYS, P_HALF), P_HALF ** -0.5),
        'peer_u': nrm((DEPTH, N_EXPERTS, D), D ** -0.5),
        'peer_v': nrm((DEPTH, N_EXPERTS, D), 0.5),
    }


def reference(x_prompt, x_sample, c_prompt, c_sample, state_lru_h, state_lru_conv, state_mlstm_C, state_mlstm_n,
              state_mlstm_m, state_mlstm_conv, w_ada, b_ada, norm_mix, norm_ffn, norm_f, lru_w_in, lru_conv_w,
              lru_conv_b, lru_w_a, lru_b_a, lru_w_i, lru_b_i, lru_lambda, lru_w_out, ml_w_in, ml_conv_w, ml_conv_b,
              ml_w_q, ml_w_k, ml_w_v, ml_w_ig, ml_b_ig, ml_w_fg, ml_b_fg, ml_skip, ml_ln_g, ml_w_out, peer_w_q,
              peer_keys, peer_u, peer_v):
    p = dict(w_ada=w_ada, b_ada=b_ada, norm_mix=norm_mix, norm_ffn=norm_ffn, norm_f=norm_f,
             lru_w_in=lru_w_in, lru_conv_w=lru_conv_w, lru_conv_b=lru_conv_b, lru_w_a=lru_w_a, lru_b_a=lru_b_a,
             lru_w_i=lru_w_i, lru_b_i=lru_b_i, lru_lambda=lru_lambda, lru_w_out=lru_w_out,
             ml_w_in=ml_w_in, ml_conv_w=ml_conv_w, ml_conv_b=ml_conv_b, ml_w_q=ml_w_q, ml_w_k=ml_w_k, ml_w_v=ml_w_v,
             ml_w_ig=ml_w_ig, ml_b_ig=ml_b_ig, ml_w_fg=ml_w_fg, ml_b_fg=ml_b_fg, ml_skip=ml_skip, ml_ln_g=ml_ln_g,
             ml_w_out=ml_w_out, peer_w_q=peer_w_q, peer_keys=peer_keys, peer_u=peer_u, peer_v=peer_v)
    B = x_prompt.shape[0]
    f32 = jnp.float32
    z_lh = jnp.zeros((N_LRU_LAYERS, B, D_RNN), f32)
    z_lc = jnp.zeros((N_LRU_LAYERS, B, CONV_W - 1, D_RNN), f32)
    z_C = jnp.zeros((N_MLSTM_LAYERS, B, M_HEADS, M_HD, M_HD), f32)
    z_n = jnp.zeros((N_MLSTM_LAYERS, B, M_HEADS, M_HD), f32)
    z_m = jnp.zeros((N_MLSTM_LAYERS, B, M_HEADS), f32)
    z_mc = jnp.zeros((N_MLSTM_LAYERS, B, CONV_W - 1, M_INNER), f32)
    y_prompt, (lh_p, lc_p, C_p, n_p, m_p, mc_p) = trunk(x_prompt, c_prompt, z_lh, z_lc, z_C, z_n, z_m, z_mc, p)
    y_sample, (lh_s, lc_s, C_s, n_s, m_s, mc_s) = trunk(x_sample, c_sample, state_lru_h, state_lru_conv, state_mlstm_C,
                                                        state_mlstm_n, state_mlstm_m, state_mlstm_conv, p)
    return (y_prompt, y_sample, lh_p, lc_p, C_p, n_p, m_p, mc_p, lh_s, lc_s, C_s, n_s, m_s, mc_s)
```

```python
import functools
import math

import jax
import jax.numpy as jnp
from jax import lax
from jax.experimental import pallas as pl
from jax.experimental.pallas import tpu as pltpu

F32 = jnp.float32
BF16 = jnp.bfloat16
EPS = 1e-6
LRU_C = 8.0
CONV_W = 4
P_TOPK = 16
LANES = 128
MXU_TILE = 256
NEG_INF = float("-inf")
NO_RANK = 99.0
VMEM_LIMIT = 52 * 1024 * 1024
_NT = (((1,), (1,)), ((), ()))
_TN = (((0,), (0,)), ((), ()))


def _cparams(*sem):
    return pltpu.CompilerParams(dimension_semantics=sem, vmem_limit_bytes=VMEM_LIMIT)


def _sigmoid(x):
    return 1.0 / (1.0 + jnp.exp(-x))


def _log_sigmoid(x):
    return jnp.minimum(x, 0.0) - jnp.log1p(jnp.exp(-jnp.abs(x)))


def _gelu(x):
    return 0.5 * x * (1.0 + jnp.tanh(math.sqrt(2.0 / math.pi) * (x + 0.044715 * (x * x * x))))


def _silu(x):
    return x * _sigmoid(x)


def _norm_mod(x, g, sc, sh):
    ms = jnp.mean(x * x, axis=-1, keepdims=True)
    return x * lax.rsqrt(ms + EPS) * g * (1.0 + sc) + sh


class _Group:
    def __init__(self, S, T, TB):
        self.S, self.T, self.TB, self.N = S, T, TB, S * T
        self.stepwise = T == 1
        if self.stepwise:
            assert S % TB == 0
            self.per = 1
            self.nb = S // TB
        else:
            assert T % TB == 0 and TB >= CONV_W - 1
            self.per = T // TB
            self.nb = S * self.per

    def tok(self, C, col=0):
        return pl.BlockSpec((self.TB, C), lambda i, *_: (i, col))

    def mod(self, D):
        if self.stepwise:
            return pl.BlockSpec((self.TB, D), lambda i, *_: (i, 0))
        per = self.per
        return pl.BlockSpec((None, 1, D), lambda i, *_: (i // per, 0, 0))

    def seq(self, R, C):
        if self.stepwise:
            return pl.BlockSpec((self.TB, R * C), lambda i, *_: (i, 0))
        per = self.per
        return pl.BlockSpec((None, R, C), lambda i, *_: (i // per, 0, 0))

    def seq_shape(self, R, C):
        return (self.S, R * C) if self.stepwise else (self.S, R, C)


def _full(shape):
    nd = len(shape)
    return pl.BlockSpec(shape, lambda *_: (0,) * nd)


def _ada_kernel(c_ref, w_ref, b_ref, o_ref):
    o_ref[...] = jnp.dot(c_ref[...].astype(BF16), w_ref[...].astype(BF16),
                         preferred_element_type=F32) + b_ref[...]


def _ada_call(c_all, w_ada, b_ada):
    depth, D, D6 = w_ada.shape
    S = c_all.shape[0]
    return pl.pallas_call(
        _ada_kernel,
        grid=(depth, D6 // D),
        in_specs=[pl.BlockSpec((S, D), lambda l, j: (0, 0)),
                  pl.BlockSpec((None, D, D), lambda l, j: (l, 0, j)),
                  pl.BlockSpec((None, 1, D), lambda l, j: (l, 0, j))],
        out_specs=pl.BlockSpec((None, S, D), lambda l, j: (l, 0, j)),
        out_shape=jax.ShapeDtypeStruct((depth, S, D6), F32),
        compiler_params=_cparams("arbitrary", "arbitrary"),
    )(c_all, w_ada, b_ada.reshape(depth, 1, D6))


def _in_proj_kernel(x_ref, g_ref, sc_ref, sh_ref, w_ref, o_ref):
    h = _norm_mod(x_ref[...], g_ref[...], sc_ref[...], sh_ref[...])
    o_ref[...] = jnp.dot(h.astype(BF16), w_ref[...], preferred_element_type=F32)


def _in_proj_call(grp, x, g, sc, sh, w, TN):
    D, Cout = w.shape
    return pl.pallas_call(
        _in_proj_kernel,
        grid=(grp.nb, Cout // TN),
        in_specs=[grp.tok(D), _full((1, D)), grp.mod(D), grp.mod(D),
                  pl.BlockSpec((D, TN), lambda i, j: (0, j))],
        out_specs=pl.BlockSpec((grp.TB, TN), lambda i, j: (i, j)),
        out_shape=jax.ShapeDtypeStruct((grp.N, Cout), F32),
        compiler_params=_cparams("arbitrary", "arbitrary"),
    )(x, g, sc, sh, w)


def _conv_chunked(x, ext_sc, cw_ref, cb_ref, first, TB):
    @pl.when(first)
    def _():
        ext_sc[pl.ds(0, 8), :] = jnp.zeros((8, x.shape[1]), F32)
    ext_sc[pl.ds(8, TB), :] = x
    y = cb_ref[...] + cw_ref[CONV_W - 1:CONV_W, :] * x
    for j in range(CONV_W - 1):
        y = y + cw_ref[j:j + 1, :] * ext_sc[pl.ds(8 - (CONV_W - 1) + j, TB), :]
    tail = ext_sc[pl.ds(TB, 8), :]
    ext_sc[pl.ds(0, 8), :] = tail
    return y


def _conv_stepwise(x, buf_ref, cw_ref, cb_ref, C):
    y = cb_ref[...] + cw_ref[CONV_W - 1:CONV_W, :] * x
    for j in range(CONV_W - 1):
        y = y + cw_ref[j:j + 1, :] * buf_ref[:, j * C:(j + 1) * C]
    return y


def _lru_kernel(stepwise, per, TB, C, *refs):
    if stepwise:
        (proj_ref, h0_ref, buf_ref, cw_ref, cb_ref, wg_ref, bg_ref, lam_ref,
         gated_ref, hT_ref, cs_ref, a_sc, b_sc) = refs
    else:
        (proj_ref, cw_ref, cb_ref, wg_ref, bg_ref, lam_ref,
         gated_ref, hT_ref, cs_ref, a_sc, b_sc, ext_sc, h_sc) = refs
    x = proj_ref[:, C:2 * C]
    if stepwise:
        xr = _conv_stepwise(x, buf_ref, cw_ref, cb_ref, C)
        cs_ref[:, 0:(CONV_W - 2) * C] = buf_ref[:, C:(CONV_W - 1) * C]
        cs_ref[:, (CONV_W - 2) * C:] = x
    else:
        first = pl.program_id(0) % per == 0
        xr = _conv_chunked(x, ext_sc, cw_ref, cb_ref, first, TB)
        cs_ref[...] = proj_ref[pl.ds(TB - (CONV_W - 1), CONV_W - 1), C:2 * C]

        @pl.when(first)
        def _():
            h_sc[...] = jnp.zeros_like(h_sc)

    T2 = MXU_TILE
    for g in range(C // T2):
        cols = slice(g * T2, (g + 1) * T2)
        xg = xr[:, cols]
        pre = jnp.dot(xg.astype(BF16), wg_ref[g], preferred_element_type=F32) + bg_ref[g]
        r = _sigmoid(pre[:, :T2])
        gi = _sigmoid(pre[:, T2:])
        log_a = LRU_C * r * _log_sigmoid(lam_ref[:, cols])
        a = jnp.exp(log_a)
        a_sc[:, cols] = a
        b_sc[:, cols] = jnp.sqrt(-jnp.tanh(log_a) * (a * a + 1.0)) * (gi * xg)

    if stepwise:
        hs = a_sc[...] * h0_ref[...] + b_sc[...]
        hT_ref[...] = hs
    else:
        def body(t, h):
            h = a_sc[pl.ds(t, 1), :] * h + b_sc[pl.ds(t, 1), :]
            b_sc[pl.ds(t, 1), :] = h
            return h
        h = lax.fori_loop(0, TB, body, h_sc[...], unroll=8)
        h_sc[...] = h
        hT_ref[...] = h
        hs = b_sc[...]
    gated_ref[...] = (hs * _gelu(proj_ref[:, 0:C])).astype(gated_ref.dtype)


def _lru_call(grp, proj, h0, buf, conv_w, conv_b, wg, bg, lam):
    C = lam.shape[-1]
    TB = grp.TB
    nt = C // MXU_TILE
    w_specs = [_full((CONV_W, C)), _full((1, C)), _full((nt, MXU_TILE, 2 * MXU_TILE)),
               _full((nt, 1, 2 * MXU_TILE)), _full((1, C))]
    scratch = [pltpu.VMEM((TB, C), F32), pltpu.VMEM((TB, C), F32)]
    if grp.stepwise:
        in_specs = [grp.tok(2 * C), grp.tok(C), grp.seq(CONV_W - 1, C)] + w_specs
        args = (proj, h0, buf, conv_w, conv_b, wg, bg, lam)
    else:
        in_specs = [grp.tok(2 * C)] + w_specs
        args = (proj, conv_w, conv_b, wg, bg, lam)
        scratch += [pltpu.VMEM((TB + 8, C), F32), pltpu.VMEM((1, C), F32)]
    return pl.pallas_call(
        functools.partial(_lru_kernel, grp.stepwise, grp.per, TB, C),
        grid=(grp.nb,),
        in_specs=in_specs,
        out_specs=[grp.tok(C), grp.seq(1, C), grp.seq(CONV_W - 1, C)],
        out_shape=[jax.ShapeDtypeStruct((grp.N, C), BF16),
                   jax.ShapeDtypeStruct(grp.seq_shape(1, C), F32),
                   jax.ShapeDtypeStruct(grp.seq_shape(CONV_W - 1, C), F32)],
        scratch_shapes=scratch,
        compiler_params=_cparams("arbitrary"),
    )(*args)


def _out_proj_kernel(a_ref, w_ref, x_ref, g1_ref, o_ref):
    mix = jnp.dot(a_ref[...], w_ref[...], preferred_element_type=F32)
    o_ref[...] = x_ref[...] + g1_ref[...] * mix


def _out_proj_call(grp, act, w, x, g1):
    C, D = w.shape
    return pl.pallas_call(
        _out_proj_kernel,
        grid=(grp.nb,),
        in_specs=[grp.tok(C), _full((C, D)), grp.tok(D), grp.mod(D)],
        out_specs=grp.tok(D),
        out_shape=jax.ShapeDtypeStruct((grp.N, D), F32),
        compiler_params=_cparams("arbitrary"),
    )(act, w, x, g1)


def _top16(s, top_sc):
    rank = jnp.full(s.shape, NO_RANK, F32)
    for r in range(P_TOPK):
        mx = jnp.max(s, axis=0, keepdims=True)
        hit = s == mx
        rank = jnp.where(hit, float(r + 1), rank)
        s = jnp.where(hit, NEG_INF, s)
        top_sc[r:r + 1, :] = mx
    return rank


def _router_kernel(TB, x_ref, g_ref, sc_ref, sh_ref, wq_ref, keys_ref,
                   hf_ref, rank2_ref, e2_ref, n1_ref, e1_ref, s1_sc, s2_sc, ta_sc, tb_sc):
    @pl.when(pl.program_id(1) == 0)
    def _():
        hf = _norm_mod(x_ref[...], g_ref[...], sc_ref[...], sh_ref[...])
        hf_ref[...] = hf.astype(BF16)

    nk = keys_ref.shape[-2]
    half = keys_ref.shape[-1]
    qT = lax.dot_general(wq_ref[...], hf_ref[...], _NT, preferred_element_type=F32).astype(BF16)
    s1_sc[...] = jnp.dot(keys_ref[0], qT[0:half], preferred_element_type=F32)
    s2_sc[...] = jnp.dot(keys_ref[1], qT[half:2 * half], preferred_element_type=F32)
    row8 = lax.broadcasted_iota(jnp.int32, (8, LANES), 0)
    for c in range(TB // LANES):
        lanes = slice(c * LANES, (c + 1) * LANES)
        s1 = s1_sc[:, lanes]
        s2 = s2_sc[:, lanes]
        _top16(s1, ta_sc)
        rank2 = _top16(s2, tb_sc)
        A = ta_sc[...]
        B = tb_sc[...]
        pieces = [A[0:1] + B, A[1:2] + B[0:8]]
        for r in range(3, 9):
            pieces.append(jnp.where(row8 < P_TOPK // r, A[r - 1:r] + B[0:8], NEG_INF))
        pieces.append(A[8:16] + B[0:1])
        cand0 = jnp.concatenate(pieces, axis=0)
        cand = cand0
        for _ in range(P_TOPK):
            tau = jnp.max(cand, axis=0, keepdims=True)
            cand = jnp.where(cand == tau, NEG_INF, cand)
        top = A[0:1] + B[0:1]
        z = jnp.sum(jnp.where(cand0 >= tau, jnp.exp(cand0 - top), 0.0), axis=0, keepdims=True)
        n1 = jnp.zeros((nk, LANES), F32)
        for k in range(P_TOPK):
            n1 = n1 + jnp.where(s1 + B[k:k + 1] >= tau, 1.0, 0.0)
        rank2_ref[:, lanes] = rank2
        e2_ref[:, lanes] = jnp.exp(s2 - B[0:1])
        n1_ref[:, lanes] = n1
        e1_ref[:, lanes] = jnp.exp(s1 - A[0:1]) / z


def _router_call(grp, x, g, sc, sh, wqT, keys):
    D = x.shape[1]
    PH, _, NK, HALF = keys.shape
    TB, N = grp.TB, grp.N
    route_spec = pl.BlockSpec((None, NK, TB), lambda i, h: (h, 0, i))
    route_shape = jax.ShapeDtypeStruct((PH, NK, N), F32)
    return pl.pallas_call(
        functools.partial(_router_kernel, TB),
        grid=(grp.nb, PH),
        in_specs=[grp.tok(D), _full((1, D)), grp.mod(D), grp.mod(D),
                  pl.BlockSpec((2 * HALF, D), lambda i, h: (h, 0)),
                  pl.BlockSpec((None, 2, NK, HALF), lambda i, h: (h, 0, 0, 0))],
        out_specs=[grp.tok(D), route_spec, route_spec, route_spec, route_spec],
        out_shape=[jax.ShapeDtypeStruct((N, D), BF16)] + [route_shape] * 4,
        scratch_shapes=[pltpu.VMEM((NK, TB), F32), pltpu.VMEM((NK, TB), F32),
                        pltpu.VMEM((P_TOPK, LANES), F32), pltpu.VMEM((P_TOPK, LANES), F32)],
        compiler_params=_cparams("arbitrary", "arbitrary"),
    )(x, g, sc, sh, wqT, keys)


def _peer_kernel(TB, EB, NK, PH, hf_ref, u_ref, vT_ref, rank2_ref, e2_ref, n1_ref, e1_ref,
                 x_ref, g2_ref, o_ref, acc_sc, s_sc, w_sc):
    e = pl.program_id(1)

    @pl.when(e == 0)
    def _():
        acc_sc[...] = jnp.zeros_like(acc_sc)

    s_sc[...] = lax.dot_general(u_ref[...], hf_ref[...], _NT, preferred_element_type=F32)
    ni = EB // NK
    i0 = pl.multiple_of(e * ni, ni)
    for c in range(TB // LANES):
        lanes = slice(c * LANES, (c + 1) * LANES)
        n1 = [n1_ref[h, pl.ds(i0, ni), lanes] for h in range(PH)]
        e1 = [e1_ref[h, pl.ds(i0, ni), lanes] for h in range(PH)]
        for il in range(ni):
            rows = slice(il * NK, (il + 1) * NK)
            gate = jnp.zeros((NK, LANES), F32)
            for h in range(PH):
                hit = rank2_ref[h, :, lanes] <= n1[h][il:il + 1]
                gate = gate + jnp.where(hit, e2_ref[h, :, lanes], 0.0) * e1[h][il:il + 1]
            w_sc[rows, lanes] = (_gelu(s_sc[rows, lanes]) * gate).astype(BF16)
    acc_sc[...] += jnp.dot(vT_ref[...], w_sc[...], preferred_element_type=F32)

    @pl.when(e == pl.num_programs(1) - 1)
    def _():
        o_ref[...] = x_ref[...] + g2_ref[...] * acc_sc[...].T


def _peer_call(grp, hf, u, vT, rank2, e2, n1, e1, x, g2, EB):
    NE, D = u.shape
    PH, NK, N = rank2.shape
    TB = grp.TB
    route_spec = pl.BlockSpec((PH, NK, TB), lambda i, e: (0, 0, i))
    return pl.pallas_call(
        functools.partial(_peer_kernel, TB, EB, NK, PH),
        grid=(grp.nb, NE // EB),
        in_specs=[grp.tok(D),
                  pl.BlockSpec((EB, D), lambda i, e: (e, 0)),
                  pl.BlockSpec((D, EB), lambda i, e: (0, e)),
                  route_spec, route_spec, route_spec, route_spec,
                  grp.tok(D), grp.mod(D)],
        out_specs=grp.tok(D),
        out_shape=jax.ShapeDtypeStruct((N, D), F32),
        scratch_shapes=[pltpu.VMEM((D, TB), F32), pltpu.VMEM((EB, TB), F32), pltpu.VMEM((EB, TB), BF16)],
        compiler_params=_cparams("arbitrary", "arbitrary"),
    )(hf, u, vT, rank2, e2, n1, e1, x, g2)


def _ml_pre_kernel(stepwise, per, TB, C, HD, *refs):
    if stepwise:
        (xm_ref, buf_ref, cw_ref, cb_ref, wq_ref, wk_ref, wv_ref, wg_ref, bg_ref,
         xc_ref, q_ref, k_ref, v_ref, gt_ref, cs_ref) = refs
    else:
        (xm_ref, cw_ref, cb_ref, wq_ref, wk_ref, wv_ref, wg_ref, bg_ref,
         xc_ref, q_ref, k_ref, v_ref, gt_ref, cs_ref, ext_sc) = refs
    xm = xm_ref[...]
    if stepwise:
        xc = _conv_stepwise(xm, buf_ref, cw_ref, cb_ref, C)
        cs_ref[:, 0:(CONV_W - 2) * C] = buf_ref[:, C:(CONV_W - 1) * C]
        cs_ref[:, (CONV_W - 2) * C:] = xm
    else:
        first = pl.program_id(0) % per == 0
        xc = _conv_chunked(xm, ext_sc, cw_ref, cb_ref, first, TB)
        cs_ref[...] = xm_ref[pl.ds(TB - (CONV_W - 1), CONV_W - 1), :]
    xc = _silu(xc)
    xc_ref[...] = xc
    T2 = MXU_TILE
    kscale = HD ** -0.5
    for g in range(C // T2):
        cols = slice(g * T2, (g + 1) * T2)
        xcg = xc[:, cols].astype(BF16)
        q_ref[:, cols] = jnp.dot(xcg, wq_ref[g], preferred_element_type=F32).astype(BF16)
        k_ref[:, cols] = (jnp.dot(xcg, wk_ref[g], preferred_element_type=F32) * kscale).astype(BF16)
        v_ref[:, cols] = jnp.dot(xm[:, cols].astype(BF16), wv_ref[g], preferred_element_type=F32).astype(BF16)
    pre = (jnp.dot(q_ref[...], wg_ref[0], preferred_element_type=F32)
           + jnp.dot(k_ref[...], wg_ref[1], preferred_element_type=F32)
           + jnp.dot(v_ref[...], wg_ref[2], preferred_element_type=F32) + bg_ref[...])
    H = C // HD
    lane = lax.broadcasted_iota(jnp.int32, pre.shape, 1)
    gt_ref[...] = jnp.where((lane >= H) & (lane < 2 * H), _log_sigmoid(pre), pre)


def _ml_pre_call(grp, proj, buf, conv_w, conv_b, wq, wk, wv, wg, bg, HD):
    C = conv_w.shape[-1]
    TB = grp.TB
    nt = C // MXU_TILE
    bd = _full((nt, MXU_TILE, MXU_TILE))
    w_specs = [_full((CONV_W, C)), _full((1, C)), bd, bd, bd, _full((3, C, LANES)), _full((1, LANES))]
    if grp.stepwise:
        in_specs = [grp.tok(C), grp.seq(CONV_W - 1, C)] + w_specs
        args = (proj, buf, conv_w, conv_b, wq, wk, wv, wg, bg)
        scratch = []
    else:
        in_specs = [grp.tok(C)] + w_specs
        args = (proj, conv_w, conv_b, wq, wk, wv, wg, bg)
        scratch = [pltpu.VMEM((TB + 8, C), F32)]
    tokC = jax.ShapeDtypeStruct((grp.N, C), BF16)
    return pl.pallas_call(
        functools.partial(_ml_pre_kernel, grp.stepwise, grp.per, TB, C, HD),
        grid=(grp.nb,),
        in_specs=in_specs,
        out_specs=[grp.tok(C), grp.tok(C), grp.tok(C), grp.tok(C), grp.tok(LANES), grp.seq(CONV_W - 1, C)],
        out_shape=[jax.ShapeDtypeStruct((grp.N, C), F32), tokC, tokC, tokC,
                   jax.ShapeDtypeStruct((grp.N, LANES), F32),
                   jax.ShapeDtypeStruct(grp.seq_shape(CONV_W - 1, C), F32)],
        scratch_shapes=scratch,
        compiler_params=_cparams("arbitrary"),
    )(*args)


def _ml_cell_kernel(L, H, HD, q_ref, k_ref, v_ref, gt_ref, hc_ref, C_ref, n_ref, m_ref, C_sc, n_sc, m_sc):
    c = pl.program_id(1)

    @pl.when(c == 0)
    def _():
        C_sc[...] = jnp.zeros_like(C_sc)
        n_sc[...] = jnp.zeros_like(n_sc)
        m_sc[...] = jnp.zeros_like(m_sc)

    G = gt_ref[...]
    row = lax.broadcasted_iota(jnp.int32, (L, L), 0)
    col = lax.broadcasted_iota(jnp.int32, (L, L), 1)
    causal = row >= col
    tril = jnp.where(causal, 1.0, 0.0).astype(BF16)
    g_hi = G.astype(BF16)
    r1 = G - g_hi.astype(F32)
    g_mid = r1.astype(BF16)
    g_lo = (r1 - g_mid.astype(F32)).astype(BF16)
    Bc = (jnp.dot(tril, g_hi, preferred_element_type=F32) + jnp.dot(tril, g_mid, preferred_element_type=F32)
          + jnp.dot(tril, g_lo, preferred_element_type=F32))
    GT = G.T
    BcT = Bc.T
    for h in range(H):
        cols = slice(h * HD, (h + 1) * HD)
        b_col = Bc[:, H + h:H + h + 1]
        b_row = BcT[H + h:H + h + 1, :]
        i_col = G[:, h:h + 1]
        i_row = GT[h:h + 1, :]
        m = m_sc[h:h + 1, 0:1]
        dmat = jnp.where(causal, b_col - b_row + i_row, NEG_INF)
        m_inter = b_col + m
        m_j = jnp.maximum(m_inter, jnp.max(dmat, axis=1, keepdims=True))
        w = jnp.exp(dmat - m_j)
        s_inter = jnp.exp(m_inter - m_j)
        qh = q_ref[:, cols]
        kh = k_ref[:, cols]
        vh = v_ref[:, cols]
        s = lax.dot_general(qh, kh, _NT, preferred_element_type=F32) * w
        n_row = n_sc[h:h + 1, :]
        num = (s_inter * jnp.dot(qh, C_sc[h].astype(BF16), preferred_element_type=F32)
               + jnp.dot(s.astype(BF16), vh, preferred_element_type=F32))
        den = (s_inter * jnp.sum(qh.astype(F32) * n_row, axis=1, keepdims=True)
               + jnp.sum(s, axis=1, keepdims=True))
        hc_ref[:, cols] = num / jnp.maximum(jnp.abs(den), jnp.exp(-m_j))
        b_last = b_col[L - 1:L, :]
        m_new = jnp.maximum(b_last + m, jnp.max(b_last - b_row + i_row, axis=1, keepdims=True))
        wg = jnp.exp(b_last - b_col + i_col - m_new)
        decay = jnp.exp(b_last + m - m_new)
        kw = kh.astype(F32) * wg
        C_sc[h] = decay * C_sc[h] + lax.dot_general(kw.astype(BF16), vh, _TN, preferred_element_type=F32)
        n_sc[h:h + 1, :] = decay * n_row + jnp.sum(kw, axis=0, keepdims=True)
        m_sc[h:h + 1, :] = jnp.broadcast_to(m_new, (1, LANES))

    @pl.when(c == pl.num_programs(1) - 1)
    def _():
        C_ref[...] = C_sc[...]
        n_ref[...] = n_sc[...]
        m_ref[...] = m_sc[...]


def _ml_cell_call(S, T, L, H, HD, q, k, v, gt):
    C = H * HD
    nc = T // L
    tok = lambda w: pl.BlockSpec((L, w), lambda s, c: (s * nc + c, 0))
    return pl.pallas_call(
        functools.partial(_ml_cell_kernel, L, H, HD),
        grid=(S, nc),
        in_specs=[tok(C), tok(C), tok(C), tok(LANES)],
        out_specs=[tok(C),
                   pl.BlockSpec((None, H, HD, HD), lambda s, c: (s, 0, 0, 0)),
                   pl.BlockSpec((None, H, HD), lambda s, c: (s, 0, 0)),
                   pl.BlockSpec((None, H, LANES), lambda s, c: (s, 0, 0))],
        out_shape=[jax.ShapeDtypeStruct((S * T, C), F32),
                   jax.ShapeDtypeStruct((S, H, HD, HD), F32),
                   jax.ShapeDtypeStruct((S, H, HD), F32),
                   jax.ShapeDtypeStruct((S, H, LANES), F32)],
        scratch_shapes=[pltpu.VMEM((H, HD, HD), F32), pltpu.VMEM((H, HD), F32), pltpu.VMEM((H, LANES), F32)],
        compiler_params=_cparams("arbitrary", "arbitrary"),
    )(q, k, v, gt)


def _ml_step_kernel(HD, q_ref, k_ref, v_ref, sc_ref, C0_ref, n0_ref, hc_ref, C_ref, n_ref, m_ref):
    q = q_ref[...].astype(F32)
    k = k_ref[...].astype(F32)
    v = v_ref[...].astype(F32)
    i_pre = sc_ref[:, 0:1]
    logf = sc_ref[:, 1:2]
    m0 = sc_ref[:, 2:3]
    m_inter = logf + m0
    m_new = jnp.maximum(m_inter, i_pre)
    w = jnp.exp(i_pre - m_new)
    decay = jnp.exp(m_inter - m_new)
    reps = HD // LANES
    q_bc = jnp.concatenate([jnp.broadcast_to(q, (LANES, HD)).T] * reps, axis=1)
    k_bc = jnp.concatenate([jnp.broadcast_to(k, (LANES, HD)).T] * reps, axis=1)
    C0 = C0_ref[...]
    n0 = n0_ref[...]
    s = jnp.sum(q * k, axis=1, keepdims=True) * w
    num = decay * jnp.sum(C0 * q_bc, axis=0, keepdims=True) + s * v
    den = decay * jnp.sum(q * n0, axis=1, keepdims=True) + s
    hc_ref[...] = num / jnp.maximum(jnp.abs(den), jnp.exp(-m_new))
    C_ref[...] = decay * C0 + (w * k_bc) * v
    n_ref[...] = decay * n0 + w * k
    m_ref[...] = jnp.broadcast_to(m_new, (1, LANES))


def _ml_step_call(S, H, HD, q, k, v, scal, C0, n0):
    SH = S * H
    vec = pl.BlockSpec((None, 1, HD), lambda i: (i, 0, 0))
    sca = pl.BlockSpec((None, 1, LANES), lambda i: (i, 0, 0))
    mat = pl.BlockSpec((None, HD, HD), lambda i: (i, 0, 0))
    return pl.pallas_call(
        functools.partial(_ml_step_kernel, HD),
        grid=(SH,),
        in_specs=[vec, vec, vec, sca, mat, vec],
        out_specs=[vec, mat, vec, sca],
        out_shape=[jax.ShapeDtypeStruct((SH, 1, HD), F32), jax.ShapeDtypeStruct((SH, HD, HD), F32),
                   jax.ShapeDtypeStruct((SH, 1, HD), F32), jax.ShapeDtypeStruct((SH, 1, LANES), F32)],
        compiler_params=_cparams("arbitrary"),
    )(q, k, v, scal, C0, n0)


def _ml_post_kernel(H, HD, hc_ref, xc_ref, z_ref, lng_ref, skip_ref, w_ref, x_ref, g1_ref, o_ref, y_sc):
    for h in range(H):
        cols = slice(h * HD, (h + 1) * HD)
        hc = hc_ref[:, cols]
        mu = jnp.mean(hc, axis=-1, keepdims=True)
        d = hc - mu
        var = jnp.mean(d * d, axis=-1, keepdims=True)
        hn = d * lax.rsqrt(var + EPS) * lng_ref[:, cols]
        y = (hn + skip_ref[:, cols] * xc_ref[:, cols]) * _silu(z_ref[:, cols])
        y_sc[:, cols] = y.astype(BF16)
    mix = jnp.dot(y_sc[...], w_ref[...], preferred_element_type=F32)
    o_ref[...] = x_ref[...] + g1_ref[...] * mix


def _ml_post_call(grp, H, HD, hc, xc, proj, ln_g, skip, w, x, g1):
    C, D = w.shape
    return pl.pallas_call(
        functools.partial(_ml_post_kernel, H, HD),
        grid=(grp.nb,),
        in_specs=[grp.tok(C), grp.tok(C), grp.tok(C, col=1), _full((1, C)), _full((1, C)),
                  _full((C, D)), grp.tok(D), grp.mod(D)],
        out_specs=grp.tok(D),
        out_shape=jax.ShapeDtypeStruct((grp.N, D), F32),
        scratch_shapes=[pltpu.VMEM((grp.TB, C), BF16)],
        compiler_params=_cparams("arbitrary"),
    )(hc, xc, proj, ln_g, skip, w, x, g1)


def _final_kernel(x_ref, g_ref, o_ref):
    x = x_ref[...]
    ms = jnp.mean(x * x, axis=-1, keepdims=True)
    o_ref[...] = x * lax.rsqrt(ms + EPS) * g_ref[...]


def _final_call(grp, x, g):
    D = x.shape[1]
    return pl.pallas_call(
        _final_kernel,
        grid=(grp.nb,),
        in_specs=[grp.tok(D), _full((1, D))],
        out_specs=grp.tok(D),
        out_shape=jax.ShapeDtypeStruct((grp.N, D), F32),
        compiler_params=_cparams("arbitrary"),
    )(x, g)


def _block_diag_tiles(w, tile):
    nb, bi, bo = w.shape
    per = tile // bi
    w4 = w.reshape(nb // per, per, bi, bo)
    eye = jnp.eye(per, dtype=w.dtype)
    return jnp.einsum("gbio,bc->gbico", w4, eye).reshape(nb // per, tile, per * bo)


def _prep(p):
    q = {}
    D = p["norm_f"].shape[0]
    q["lru_w_in"] = p["lru_w_in"].astype(BF16)
    q["lru_w_out"] = p["lru_w_out"].astype(BF16)
    wa = jax.vmap(lambda w: _block_diag_tiles(w, MXU_TILE))(p["lru_w_a"])
    wi = jax.vmap(lambda w: _block_diag_tiles(w, MXU_TILE))(p["lru_w_i"])
    q["lru_wg"] = jnp.concatenate([wa, wi], axis=-1).astype(BF16)
    nt = wa.shape[1]
    na = wa.shape[0]
    q["lru_bg"] = jnp.concatenate([p["lru_b_a"].reshape(na, nt, 1, MXU_TILE),
                                   p["lru_b_i"].reshape(na, nt, 1, MXU_TILE)], axis=-1)
    q["ml_w_in"] = p["ml_w_in"].astype(BF16)
    q["ml_w_out"] = p["ml_w_out"].astype(BF16)
    for name in ("ml_w_q", "ml_w_k", "ml_w_v"):
        q[name] = jax.vmap(lambda w: _block_diag_tiles(w, MXU_TILE))(p[name]).astype(BF16)
    nb_, c3, H = p["ml_w_ig"].shape
    C = c3 // 3
    wg = jnp.concatenate([p["ml_w_ig"], p["ml_w_fg"]], axis=-1)
    wg = jnp.pad(wg, ((0, 0), (0, 0), (0, LANES - 2 * H)))
    q["ml_wg"] = wg.reshape(nb_, 3, C, LANES).astype(BF16)
    q["ml_bg"] = jnp.pad(jnp.concatenate([p["ml_b_ig"], p["ml_b_fg"]], axis=-1),
                         ((0, 0), (0, LANES - 2 * H))).reshape(nb_, 1, LANES)
    q["peer_wqT"] = jnp.swapaxes(p["peer_w_q"], 1, 2).astype(BF16)
    q["peer_keys"] = p["peer_keys"].astype(BF16)
    q["peer_u"] = p["peer_u"].astype(BF16)
    q["peer_vT"] = jnp.swapaxes(p["peer_v"], 1, 2).astype(BF16)
    return q


def _trunk(grp, peer_grp, x, mod, st, p, q, cfg):
    D = x.shape[1]
    S, T = grp.S, grp.T
    H, HD = cfg["H"], cfg["HD"]
    depth = p["norm_mix"].shape[0]
    outs = {}
    row = lambda a: a.reshape(1, -1)
    for l in range(depth):
        sh1, sc1, g1, sh2, sc2, g2 = mod[l]
        j = l // 2
        if l % 2 == 0:
            C = p["lru_lambda"].shape[-1]
            proj = _in_proj_call(grp, x, row(p["norm_mix"][l]), sc1, sh1, q["lru_w_in"][j], 2 * C)
            h0 = st["lru_h"][j] if grp.stepwise else None
            buf = st["lru_conv"][j].reshape(S, -1) if grp.stepwise else None
            gated, hT, cs = _lru_call(grp, proj, h0, buf, p["lru_conv_w"][j], row(p["lru_conv_b"][j]),
                                      q["lru_wg"][j], q["lru_bg"][j], row(p["lru_lambda"][j]))
            outs.setdefault("lru_h", []).append(hT.reshape(S, C))
            outs.setdefault("lru_conv", []).append(cs.reshape(S, CONV_W - 1, C))
            x = _out_proj_call(grp, gated, q["lru_w_out"][j], x, g1)
        else:
            C = H * HD
            proj = _in_proj_call(grp, x, row(p["norm_mix"][l]), sc1, sh1, q["ml_w_in"][j], C)
            buf = st["ml_conv"][j].reshape(S, -1) if grp.stepwise else None
            xc, qq, kk, vv, gt, cs = _ml_pre_call(grp, proj, buf, p["ml_conv_w"][j], row(p["ml_conv_b"][j]),
                                                  q["ml_w_q"][j], q["ml_w_k"][j], q["ml_w_v"][j],
                                                  q["ml_wg"][j], q["ml_bg"][j], HD)
            if grp.stepwise:
                scal = jnp.stack([gt[:, 0:H], gt[:, H:2 * H], st["ml_m"][j]], axis=-1)
                scal = jnp.pad(scal, ((0, 0), (0, 0), (0, LANES - 3))).reshape(S * H, 1, LANES)
                vec = lambda a: a.reshape(S * H, 1, HD)
                hc, Cn, nn, mm = _ml_step_call(S, H, HD, vec(qq), vec(kk), vec(vv), scal,
                                               st["ml_C"][j].reshape(S * H, HD, HD), vec(st["ml_n"][j]))
                hc = hc.reshape(S, C)
                mm = mm.reshape(S, H, LANES)
            else:
                hc, Cn, nn, mm = _ml_cell_call(S, T, cfg["L"], H, HD, qq, kk, vv, gt)
            outs.setdefault("ml_C", []).append(Cn.reshape(S, H, HD, HD))
            outs.setdefault("ml_n", []).append(nn.reshape(S, H, HD))
            outs.setdefault("ml_m", []).append(mm[:, :, 0])
            outs.setdefault("ml_conv", []).append(cs.reshape(S, CONV_W - 1, C))
            x = _ml_post_call(grp, H, HD, hc, xc, proj, row(p["ml_ln_g"][j]), row(p["ml_skip"][j]),
                              q["ml_w_out"][j], x, g1)
        hf, rank2, e2, n1, e1 = _router_call(peer_grp, x, row(p["norm_ffn"][l]), sc2, sh2,
                                             q["peer_wqT"][l], q["peer_keys"][l])
        x = _peer_call(peer_grp, hf, q["peer_u"][l], q["peer_vT"][l], rank2, e2, n1, e1, x, g2, cfg["EB"])
    y = _final_call(grp, x, row(p["norm_f"]))
    return y, tuple(jnp.stack(outs[k]) for k in ("lru_h", "lru_conv", "ml_C", "ml_n", "ml_m", "ml_conv"))


def kernel(x_prompt, x_sample, c_prompt, c_sample, state_lru_h, state_lru_conv, state_mlstm_C, state_mlstm_n, state_mlstm_m, state_mlstm_conv, w_ada, b_ada, norm_mix, norm_ffn, norm_f, lru_w_in, lru_conv_w, lru_conv_b, lru_w_a, lru_b_a, lru_w_i, lru_b_i, lru_lambda, lru_w_out, ml_w_in, ml_conv_w, ml_conv_b, ml_w_q, ml_w_k, ml_w_v, ml_w_ig, ml_b_ig, ml_w_fg, ml_b_fg, ml_skip, ml_ln_g, ml_w_out, peer_w_q, peer_keys, peer_u, peer_v):
    p = dict(w_ada=w_ada, b_ada=b_ada, norm_mix=norm_mix, norm_ffn=norm_ffn, norm_f=norm_f,
             lru_w_in=lru_w_in, lru_conv_w=lru_conv_w, lru_conv_b=lru_conv_b, lru_w_a=lru_w_a, lru_b_a=lru_b_a,
             lru_w_i=lru_w_i, lru_b_i=lru_b_i, lru_lambda=lru_lambda, lru_w_out=lru_w_out,
             ml_w_in=ml_w_in, ml_conv_w=ml_conv_w, ml_conv_b=ml_conv_b, ml_w_q=ml_w_q, ml_w_k=ml_w_k, ml_w_v=ml_w_v,
             ml_w_ig=ml_w_ig, ml_b_ig=ml_b_ig, ml_w_fg=ml_w_fg, ml_b_fg=ml_b_fg, ml_skip=ml_skip, ml_ln_g=ml_ln_g,
             ml_w_out=ml_w_out, peer_w_q=peer_w_q, peer_keys=peer_keys, peer_u=peer_u, peer_v=peer_v)
    B, T, D = x_prompt.shape
    SB, ST, _ = x_sample.shape
    assert ST == 1
    depth = w_ada.shape[0]
    H, HD = state_mlstm_C.shape[2], state_mlstm_C.shape[3]
    NE = peer_u.shape[1]
    q = _prep(p)
    NK = peer_keys.shape[3]
    cfg = dict(H=H, HD=HD, L=min(T, 256), EB=8 * NK)

    mod = _ada_call(jnp.concatenate([c_prompt, c_sample], axis=0), w_ada, b_ada)
    mod = mod.reshape(depth, B + SB, 6, D)
    mod_p = [[mod[l, :B, k].reshape(B, 1, D) for k in range(6)] for l in range(depth)]
    mod_s = [[mod[l, B:, k] for k in range(6)] for l in range(depth)]

    gp = _Group(B, T, min(T, 256))
    gpp = _Group(B, T, min(T, 512))
    gs = _Group(SB, 1, SB)
    y_p, st_p = _trunk(gp, gpp, x_prompt.reshape(B * T, D), mod_p, None, p, q, cfg)
    st = dict(lru_h=state_lru_h, lru_conv=state_lru_conv, ml_C=state_mlstm_C, ml_n=state_mlstm_n,
              ml_m=state_mlstm_m, ml_conv=state_mlstm_conv)
    y_s, st_s = _trunk(gs, gs, x_sample.reshape(SB, D), mod_s, st, p, q, cfg)
    return (y_p.reshape(B, T, D).astype(x_prompt.dtype), y_s.reshape(SB, 1, D).astype(x_sample.dtype)) + st_p + st_s
```
